```python
import math
import jax
import jax.numpy as jnp
from jax import lax
import numpy as np

D_MODEL = 2048
BATCH = 2
SEQ = 4096
DEPTH = 2

GRID_W = 64
CTX_LEN = 256
EPS = 1e-6
ROPE_THETA = 10000.0
NEG_INF = -1e30

BR_W = D_MODEL // 2
N_BRANCH = 4

HY_W = BR_W
HY_SHORT = 3
HY_EMB = 33
HY_FILT = 64
HY_FAST_PCT = 0.3
HY_SLOW_PCT = 1.5
HY_TARGET = 1e-2

SWA_HD = 128
SWA_HEADS = BR_W // SWA_HD
SWA_KV = SWA_HEADS // 4
SWA_WINDOW = 128
SWA_BLOCK = 128

LRU_W = BR_W
LRU_HEADS = 8
LRU_CONV = 4
LRU_PAD = 2
LRU_C = 8.0

RET_DK = 256
RET_DV = 256
RET_HEADS = BR_W // RET_DV
RET_CHUNK = 128

N_EXPERTS = 64
TOP_K = 8
EXP_FF = 512
SH_FF = 512
ROUTED_SCALE = 2.5
MOE_BLOCK = 128

SPLITS = (3 * HY_W, SWA_HEADS * SWA_HD, SWA_KV * SWA_HD, SWA_KV * SWA_HD, LRU_W, LRU_W,
          RET_HEADS * RET_DK, RET_HEADS * RET_DK, RET_HEADS * RET_DV, RET_HEADS * RET_DV,
          N_BRANCH * D_MODEL)
IN_COLS = sum(SPLITS)

kernel_name = 'hybrid_bidir_diffusion_trunk'


def rmsnorm(x, g):
    x32 = x.astype(jnp.float32)
    y = x32 * lax.rsqrt(jnp.mean(x32 * x32, axis=-1, keepdims=True) + EPS)
    return y.astype(x.dtype) * g


def modulate(x, shift, scale):
    return x * (1 + scale) + shift


def split_cols(a, sizes):
    return jnp.split(a, np.cumsum(sizes)[:-1].tolist(), axis=-1)


def dwconv(x, w, b, pad_left):
    K, C = w.shape
    y = lax.conv_general_dilated(x, w[:, None, :].astype(x.dtype), window_strides=(1,),
                                 padding=[(pad_left, K - 1 - pad_left)],
                                 dimension_numbers=('NWC', 'WIO', 'NWC'), feature_group_count=C)
    return y + b


def rope_angles(pos, dim):
    inv = ROPE_THETA ** (-jnp.arange(0, dim, 2, dtype=jnp.float32) / dim)
    return pos.astype(jnp.float32)[:, None] * inv[None, :]


def apply_rope(x, ang):
    cos = jnp.cos(ang)[None, :, None, :].astype(x.dtype)
    sin = jnp.sin(ang)[None, :, None, :].astype(x.dtype)
    x1, x2 = jnp.split(x, 2, axis=-1)
    return jnp.concatenate([x1 * cos - x2 * sin, x1 * sin + x2 * cos], axis=-1)


def axial_angles(L, hd):
    rows = L // GRID_W
    row = jnp.repeat(jnp.arange(rows), GRID_W)
    col = jnp.tile(jnp.arange(GRID_W), rows)
    return jnp.concatenate([rope_angles(row, hd // 2), rope_angles(col, hd // 2)], axis=-1)


def hyena_filters(L, w1, b1, w2, b2, w3, b3, freq, w_out):
    f32 = jnp.float32
    w1, b1, w2, b2, w3, b3, freq, w_out = [a.astype(f32) for a in (w1, b1, w2, b2, w3, b3, freq, w_out)]
    t = jnp.linspace(0.0, 1.0, L, dtype=f32)[:, None]
    bands = (HY_EMB - 1) // 2
    w = 2.0 * math.pi * jnp.arange(L, dtype=f32)[:, None] / L
    f = jnp.linspace(1e-4, bands - 1, bands, dtype=f32)[None, :]
    z = jnp.concatenate([t, jnp.cos(f * w), -jnp.sin(f * w)], axis=-1)
    h = jnp.sin(freq * (z @ w1 + b1))
    h = jnp.sin(freq * (h @ w2 + b2))
    h = jnp.sin(freq * (h @ w3 + b3))
    h = (h @ w_out).reshape(L, 2, 2, HY_W)
    deltas = jnp.abs(jnp.linspace(math.log(HY_TARGET) / HY_SLOW_PCT, math.log(HY_TARGET) / HY_FAST_PCT,
                                  2 * HY_W, dtype=f32)).reshape(2, 1, HY_W)
    h = h * jnp.exp(-t[:, :, None, None] * deltas[None])
    fwd = h[:, :, 0]
    bwd = h[:0:-1, :, 1]
    k = jnp.concatenate([fwd, jnp.zeros((1, 2, HY_W), f32), bwd], axis=0)
    return k / jnp.sum(jnp.abs(k), axis=0, keepdims=True)


def fft_conv(z, kf, L):
    zf = jnp.fft.rfft(z.astype(jnp.float32), n=2 * L, axis=1)
    return jnp.fft.irfft(zf * kf[None], n=2 * L, axis=1)[:, :L].astype(z.dtype)


def hyena_mixer(proj, conv_w, conv_b, kfilt, bias):
    L = proj.shape[1]
    uc = dwconv(proj, conv_w, conv_b, (HY_SHORT - 1) // 2)
    x1, x2, v = jnp.split(uc, 3, axis=-1)
    kf = jnp.fft.rfft(kfilt, n=2 * L, axis=0)
    z = x1 * (fft_conv(v, kf[:, 0], L) + bias[0] * v)
    return x2 * (fft_conv(z, kf[:, 1], L) + bias[1] * z)


def swa_latent(q, k, v, kc, vc, sink):
    B_, L, H, hd = q.shape
    KVH = k.shape[2]
    G = H // KVH
    W = SWA_BLOCK
    NB = L // W
    scale = hd ** -0.5
    qb = q.reshape(B_, NB, W, KVH, G, hd)

    def windows(a):
        ap = jnp.pad(a, ((0, 0), (W, W), (0, 0), (0, 0))).reshape(B_, NB + 2, W, KVH, hd)
        return jnp.concatenate([ap[:, :-2], ap[:, 1:-1], ap[:, 2:]], axis=2)

    kw, vw = windows(k), windows(v)
    s_loc = jnp.einsum('bnqkgd,bnskd->bnkgqs', qb, kw).astype(jnp.float32) * scale
    qpos = jnp.arange(NB)[:, None, None] * W + jnp.arange(W)[None, :, None]
    kpos = jnp.arange(NB)[:, None, None] * W - W + jnp.arange(3 * W)[None, None, :]
    valid = (jnp.abs(qpos - kpos) <= SWA_WINDOW) & (kpos >= 0) & (kpos < L)
    s_loc = jnp.where(valid[None, :, None, None], s_loc, NEG_INF)
    s_ctx = jnp.einsum('bnqkgd,bskd->bnkgqs', qb, kc).astype(jnp.float32) * scale
    snk = jnp.broadcast_to(sink.reshape(KVH, G)[None, None, :, :, None, None].astype(jnp.float32),
                           s_loc.shape[:-1] + (1,))
    p = jax.nn.softmax(jnp.concatenate([snk, s_loc, s_ctx], axis=-1), axis=-1)
    p_loc = p[..., 1:1 + 3 * W].astype(v.dtype)
    p_ctx = p[..., 1 + 3 * W:].astype(v.dtype)
    o = jnp.einsum('bnkgqs,bnskd->bnqkgd', p_loc, vw) + jnp.einsum('bnkgqs,bskd->bnqkgd', p_ctx, vc)
    return o.reshape(B_, L, H * hd)


def attn_ctx(qc, kc, vc, sink):
    B_, Lc, H, hd = qc.shape
    KVH = kc.shape[2]
    G = H // KVH
    qg = qc.reshape(B_, Lc, KVH, G, hd)
    s = jnp.einsum('bqkgd,bskd->bkgqs', qg, kc).astype(jnp.float32) * hd ** -0.5
    snk = jnp.broadcast_to(sink.reshape(KVH, G)[None, :, :, None, None].astype(jnp.float32),
                           s.shape[:-1] + (1,))
    p = jax.nn.softmax(jnp.concatenate([snk, s], axis=-1), axis=-1)[..., 1:].astype(vc.dtype)
    return jnp.einsum('bkgqs,bskd->bqkgd', p, vc).reshape(B_, Lc, H * hd)


def rglru_scan(xr, wa, ba, wx, bx, lam, h0, reverse):
    B_, L, C = xr.shape
    xh = xr.reshape(B_, L, LRU_HEADS, C // LRU_HEADS)
    r = jax.nn.sigmoid((jnp.einsum('blhi,hij->blhj', xh, wa).reshape(B_, L, C) + ba).astype(jnp.float32))
    i = jax.nn.sigmoid((jnp.einsum('blhi,hij->blhj', xh, wx).reshape(B_, L, C) + bx).astype(jnp.float32))
    log_a = -LRU_C * r * jax.nn.softplus(-lam.astype(jnp.float32))
    a = jnp.exp(log_a)
    b = jnp.sqrt(1.0 - jnp.exp(2.0 * log_a)) * (i * xr.astype(jnp.float32))

    def combine(e1, e2):
        a1, b1 = e1
        a2, b2 = e2
        return a1 * a2, a2 * b1 + b2

    A, Bs = lax.associative_scan(combine, (a, b), axis=1, reverse=reverse)
    return A * h0[:, None, :] + Bs


def retention_chunks(q, k, v, log_g, s0, strict):
    f32 = jnp.float32
    B_, L, H, dk = q.shape
    dv = v.shape[-1]
    C = RET_CHUNK
    N = L // C
    qc = q.astype(f32).reshape(B_, N, C, H, dk)
    kc = k.astype(f32).reshape(B_, N, C, H, dk)
    vc = v.astype(f32).reshape(B_, N, C, H, dv)
    pos = jnp.arange(C, dtype=f32)
    diff = pos[:, None] - pos[None, :]
    keep = diff > 0 if strict else diff >= 0
    dmat = jnp.where(keep[None], jnp.exp(jnp.maximum(diff, 0.0)[None] * log_g[:, None, None]), 0.0)
    s = jnp.einsum('bnqhd,bnshd->bnhqs', qc, kc) * dmat[None, None]
    intra = jnp.einsum('bnhqs,bnshe->bnqhe', s, vc)
    zeta = jnp.exp((C - 1.0 - pos)[None, :] * log_g[:, None])
    upd = jnp.einsum('bnshd,bnshe,hs->nbhde', kc, vc, zeta)
    g_chunk = jnp.exp(C * log_g)[None, :, None, None]

    def step(S, u):
        return g_chunk * S + u, S

    s_last, s_prev = lax.scan(step, s0, upd)
    xi = jnp.exp((pos + 1.0)[None, :] * log_g[:, None])
    cross = jnp.einsum('bnqhd,nbhde,hq->bnqhe', qc, s_prev, xi)
    return (intra + cross).reshape(B_, L, H, dv), s_last


def retention_bidir(q, k, v, log_g, s0f, s0b):
    of, sf = retention_chunks(q, k, v, log_g[0], s0f, False)
    ob, sb = retention_chunks(q[:, ::-1], k[:, ::-1], v[:, ::-1], log_g[1], s0b, True)
    return of + ob[:, ::-1], sf, sb


def retention_out(o, gate):
    B_, L = o.shape[:2]
    n = o * lax.rsqrt(jnp.mean(o * o, axis=-1, keepdims=True) + EPS)
    return n.reshape(B_, L, RET_HEADS * RET_DV).astype(gate.dtype) * jax.nn.silu(gate)


def merge_branches(ys, gate_logits, w_branch, w_out):
    gates = jax.nn.sigmoid(gate_logits.astype(jnp.float32)).astype(gate_logits.dtype)
    gates = jnp.split(gates, N_BRANCH, axis=-1)
    merged = gates[0] * (ys[0] @ w_branch[0])
    for b in range(1, N_BRANCH):
        merged = merged + gates[b] * (ys[b] @ w_branch[b])
    return merged @ w_out


def moe_ffn(xf, router_w, router_bias, wg, wu, wd, sg, su, sd):
    T, D = xf.shape
    E = wg.shape[0]
    dt = xf.dtype
    scores = jax.nn.sigmoid((xf @ router_w).astype(jnp.float32))
    _, idx = lax.top_k(scores + router_bias.astype(jnp.float32), TOP_K)
    w_sel = jnp.take_along_axis(scores, idx, axis=-1)
    w_sel = ROUTED_SCALE * w_sel / jnp.sum(w_sel, axis=-1, keepdims=True)
    A = T * TOP_K
    flat_e = idx.reshape(A)
    order = jnp.argsort(flat_e)
    se = flat_e[order]
    st = (jnp.arange(A, dtype=jnp.int32) // TOP_K)[order]
    sw = w_sel.reshape(A)[order]
    counts = jnp.bincount(flat_e, length=E)
    padded = (counts + MOE_BLOCK - 1) // MOE_BLOCK * MOE_BLOCK
    start = jnp.cumsum(counts) - counts
    pend = jnp.cumsum(padded)
    pstart = pend - padded
    dest = pstart[se] + jnp.arange(A, dtype=jnp.int32) - start[se]
    n_blocks = (A + E * (MOE_BLOCK - 1) + MOE_BLOCK - 1) // MOE_BLOCK
    P = n_blocks * MOE_BLOCK
    slot_tok = jnp.full((P,), T, jnp.int32).at[dest].set(st)
    slot_w = jnp.zeros((P,), jnp.float32).at[dest].set(sw)
    block_e = jnp.minimum(jnp.searchsorted(pend, jnp.arange(n_blocks, dtype=jnp.int32) * MOE_BLOCK,
                                           side='right'), E - 1)
    x_pad = jnp.concatenate([xf, jnp.zeros((1, D), dt)], axis=0)

    def block(acc, blk):
        tok, wts, e = blk
        xb = x_pad[tok]
        hb = jax.nn.silu(xb @ wg[e]) * (xb @ wu[e])
        return acc.at[tok].add((hb @ wd[e]) * wts.astype(dt)[:, None]), None

    acc, _ = lax.scan(block, jnp.zeros((T + 1, D), dt),
                      (slot_tok.reshape(n_blocks, MOE_BLOCK), slot_w.reshape(n_blocks, MOE_BLOCK), block_e))
    shared = (jax.nn.silu(xf @ sg) * (xf @ su)) @ sd
    return acc[:T] + shared


def hybrid_layer(x, ctx, mod_x, mod_c, p, update_ctx):
    B_, L, D = x.shape
    Lc = ctx.shape[1]
    dt = x.dtype
    f32 = jnp.float32
    sh1, sc1, g1, sh2, sc2, g2 = mod_x
    csh1, csc1, cg1, csh2, csc2, cg2 = mod_c
    u = modulate(rmsnorm(x, p['norm_mix']), sh1, sc1)
    uc = modulate(rmsnorm(ctx, p['norm_mix']), csh1, csc1)
    hy, sq, sk, sv, lgt, lrc, rq, rk, rv, rgt, gl = split_cols(u @ p['w_in'], SPLITS)
    hy_c, sq_c, sk_c, sv_c, lgt_c, lrc_c, rq_c, rk_c, rv_c, rgt_c, gl_c = split_cols(uc @ p['w_in'], SPLITS)

    ya = hyena_mixer(hy, p['hy_conv_w'], p['hy_conv_b'], hyena_filters(L, *p['hy_filt']), p['hy_bias'])

    ang = axial_angles(L, SWA_HD)
    q = apply_rope(sq.reshape(B_, L, SWA_HEADS, SWA_HD), ang)
    k = apply_rope(sk.reshape(B_, L, SWA_KV, SWA_HD), ang)
    v = sv.reshape(B_, L, SWA_KV, SWA_HD)
    k_c = sk_c.reshape(B_, Lc, SWA_KV, SWA_HD)
    v_c = sv_c.reshape(B_, Lc, SWA_KV, SWA_HD)
    yb = swa_latent(q, k, v, k_c, v_c, p['swa_sink'])

    xr = dwconv(lrc, p['lru_conv_w'], p['lru_conv_b'], LRU_PAD)
    xr_c = dwconv(lrc_c, p['lru_conv_w'], p['lru_conv_b'], LRU_PAD)
    h0 = jnp.zeros((B_, LRU_W), f32)
    hf_c = rglru_scan(xr_c, *p['lru_f'], h0, False)
    hb_c = rglru_scan(xr_c, *p['lru_b'], h0, True)
    hf = rglru_scan(xr, *p['lru_f'], hf_c[:, -1], False)
    hb = rglru_scan(xr, *p['lru_b'], hb_c[:, 0], True)
    yc = jax.nn.gelu(lgt) * (hf + hb).astype(dt)

    log_g = -jnp.abs(p['ret_log_decay'].astype(f32))
    ang_r = rope_angles(jnp.arange(L), RET_DK)
    kscale = RET_DK ** -0.5
    rq_l = apply_rope(rq.reshape(B_, L, RET_HEADS, RET_DK), ang_r)
    rk_l = apply_rope(rk.reshape(B_, L, RET_HEADS, RET_DK), ang_r) * kscale
    rv_l = rv.reshape(B_, L, RET_HEADS, RET_DV)
    s0 = jnp.zeros((B_, RET_HEADS, RET_DK, RET_DV), f32)
    o_c, sf_c, sb_c = retention_bidir(rq_c.reshape(B_, Lc, RET_HEADS, RET_DK),
                                      rk_c.reshape(B_, Lc, RET_HEADS, RET_DK) * kscale,
                                      rv_c.reshape(B_, Lc, RET_HEADS, RET_DV), log_g, s0, s0)
    o_l, _, _ = retention_bidir(rq_l, rk_l, rv_l, log_g, sf_c, sb_c)
    yd = retention_out(o_l, rgt)

    x = x + g1 * merge_branches((ya, yb, yc, yd), gl, p['w_branch'], p['w_out'])
    h = modulate(rmsnorm(x, p['norm_ffn']), sh2, sc2).reshape(B_ * L, D)

    if update_ctx:
        ya_c = hyena_mixer(hy_c, p['hy_conv_w'], p['hy_conv_b'], hyena_filters(Lc, *p['hy_filt']), p['hy_bias'])
        yb_c = attn_ctx(sq_c.reshape(B_, Lc, SWA_HEADS, SWA_HD), k_c, v_c, p['swa_sink'])
        yc_c = jax.nn.gelu(lgt_c) * (hf_c + hb_c).astype(dt)
        yd_c = retention_out(o_c, rgt_c)
        ctx = ctx + cg1 * merge_branches((ya_c, yb_c, yc_c, yd_c), gl_c, p['w_branch'], p['w_out'])
        hc = modulate(rmsnorm(ctx, p['norm_ffn']), csh2, csc2).reshape(B_ * Lc, D)
        y = moe_ffn(jnp.concatenate([h, hc], axis=0), *p['moe'])
        ctx = ctx + cg2 * y[B_ * L:].reshape(B_, Lc, D)
    else:
        y = moe_ffn(h, *p['moe'])
    x = x + g2 * y[:B_ * L].reshape(B_, L, D)
    return x, ctx


def setup_inputs(seed: int = 0) -> dict:
    key = jax.random.key(seed)
    ks = list(jax.random.split(key, 48))
    f32 = jnp.float32

    def nrm(shape, scale):
        return jax.random.normal(ks.pop(), shape, f32) * scale

    D = D_MODEL
    bw = LRU_W // LRU_HEADS
    u = jax.random.uniform(ks.pop(), (DEPTH, 2, LRU_W), f32, 0.9, 0.999)
    a = u ** (1.0 / LRU_C)
    base = jnp.log1p(-(2.0 ** (-5.0 - jnp.arange(RET_HEADS, dtype=f32))))
    return {
        'x': nrm((BATCH, SEQ, D), 1.0),
        'c': nrm((BATCH, D), 1.0),
        'ctx': nrm((BATCH, CTX_LEN, D), 1.0),
        'c_ctx': nrm((D,), 1.0),
        'ada_w': nrm((DEPTH, D, 6 * D), 0.5 * D ** -0.5),
        'ada_b': nrm((DEPTH, 6 * D), 0.02),
        'norm_mix': 1.0 + nrm((DEPTH, D), 0.02),
        'norm_ffn': 1.0 + nrm((DEPTH, D), 0.02),
        'w_in': nrm((DEPTH, D, IN_COLS), D ** -0.5),
        'hy_conv_w': nrm((DEPTH, HY_SHORT, 3 * HY_W), HY_SHORT ** -0.5),
        'hy_conv_b': nrm((DEPTH, 3 * HY_W), 0.02),
        'hy_filt_w1': nrm((DEPTH, HY_EMB, HY_FILT), HY_EMB ** -0.5),
        'hy_filt_b1': nrm((DEPTH, HY_FILT), 0.02),
        'hy_filt_w2': nrm((DEPTH, HY_FILT, HY_FILT), HY_FILT ** -0.5),
        'hy_filt_b2': nrm((DEPTH, HY_FILT), 0.02),
        'hy_filt_w3': nrm((DEPTH, HY_FILT, HY_FILT), HY_FILT ** -0.5),
        'hy_filt_b3': nrm((DEPTH, HY_FILT), 0.02),
        'hy_filt_freq': 1.0 + nrm((DEPTH, HY_FILT), 0.02),
        'hy_filt_out': nrm((DEPTH, HY_FILT, 4 * HY_W), HY_FILT ** -0.5),
        'hy_bias': nrm((DEPTH, 2, HY_W), 1.0),
        'swa_sink': nrm((DEPTH, SWA_HEADS), 1.0),
        'lru_conv_w': nrm((DEPTH, LRU_CONV, LRU_W), LRU_CONV ** -0.5),
        'lru_conv_b': nrm((DEPTH, LRU_W), 0.02),
        'lru_wa': nrm((DEPTH, 2, LRU_HEADS, bw, bw), bw ** -0.5),
        'lru_ba': nrm((DEPTH, 2, LRU_W), 0.02),
        'lru_wx': nrm((DEPTH, 2, LRU_HEADS, bw, bw), bw ** -0.5),
        'lru_bx': nrm((DEPTH, 2, LRU_W), 0.02),
        'lru_lambda': jnp.log(a) - jnp.log1p(-a),
        'ret_log_decay': base * (1.0 + nrm((DEPTH, 2, RET_HEADS), 0.05)),
        'w_branch': nrm((DEPTH, N_BRANCH, BR_W, D), BR_W ** -0.5),
        'w_out': nrm((DEPTH, D, D), D ** -0.5),
        'router_w': nrm((DEPTH, D, N_EXPERTS), D ** -0.5),
        'router_bias': nrm((DEPTH, N_EXPERTS), 0.01),
        'exp_w_gate': nrm((DEPTH, N_EXPERTS, D, EXP_FF), D ** -0.5),
        'exp_w_up': nrm((DEPTH, N_EXPERTS, D, EXP_FF), D ** -0.5),
        'exp_w_down': nrm((DEPTH, N_EXPERTS, EXP_FF, D), EXP_FF ** -0.5),
        'sh_w_gate': nrm((DEPTH, D, SH_FF), D ** -0.5),
        'sh_w_up': nrm((DEPTH, D, SH_FF), D ** -0.5),
        'sh_w_down': nrm((DEPTH, SH_FF, D), SH_FF ** -0.5),
        'final_norm': 1.0 + nrm((D,), 0.02),
    }


def reference(x, c, ctx, c_ctx, ada_w, ada_b, norm_mix, norm_ffn, w_in, hy_conv_w, hy_conv_b,
              hy_filt_w1, hy_filt_b1, hy_filt_w2, hy_filt_b2, hy_filt_w3, hy_filt_b3, hy_filt_freq,
              hy_filt_out, hy_bias, swa_sink, lru_conv_w, lru_conv_b, lru_wa, lru_ba, lru_wx, lru_bx,
              lru_lambda, ret_log_decay, w_branch, w_out, router_w, router_bias, exp_w_gate, exp_w_up,
              exp_w_down, sh_w_gate, sh_w_up, sh_w_down, final_norm):
    s_c = jax.nn.silu(c)
    s_cc = jax.nn.silu(c_ctx)
    for l in range(DEPTH):
        p = {
            'norm_mix': norm_mix[l], 'norm_ffn': norm_ffn[l], 'w_in': w_in[l],
            'hy_conv_w': hy_conv_w[l], 'hy_conv_b': hy_conv_b[l],
            'hy_filt': (hy_filt_w1[l], hy_filt_b1[l], hy_filt_w2[l], hy_filt_b2[l], hy_filt_w3[l],
                        hy_filt_b3[l], hy_filt_freq[l], hy_filt_out[l]),
            'hy_bias': hy_bias[l], 'swa_sink': swa_sink[l],
            'lru_conv_w': lru_conv_w[l], 'lru_conv_b': lru_conv_b[l],
            'lru_f': (lru_wa[l][0], lru_ba[l][0], lru_wx[l][0], lru_bx[l][0], lru_lambda[l][0]),
            'lru_b': (lru_wa[l][1], lru_ba[l][1], lru_wx[l][1], lru_bx[l][1], lru_lambda[l][1]),
            'ret_log_decay': ret_log_decay[l], 'w_branch': w_branch[l], 'w_out': w_out[l],
            'moe': (router_w[l], router_bias[l], exp_w_gate[l], exp_w_up[l], exp_w_down[l],
                    sh_w_gate[l], sh_w_up[l], sh_w_down[l]),
        }
        mod_x = jnp.split((s_c @ ada_w[l] + ada_b[l])[:, None, :], 6, axis=-1)
        mod_c = jnp.split(s_cc @ ada_w[l] + ada_b[l], 6, axis=-1)
        x, ctx = hybrid_layer(x, ctx, mod_x, mod_c, p, l < DEPTH - 1)
    return rmsnorm(x, final_norm)
```

```python
import functools
import math

import jax
import jax.numpy as jnp
import numpy as np
from jax import lax
from jax.experimental import pallas as pl
from jax.experimental.pallas import tpu as pltpu

D_MODEL = 2048
BATCH = 2
SEQ = 4096
DEPTH = 2
GRID_W = 64
CTX_LEN = 256
EPS = 1e-6
ROPE_THETA = 10000.0
NEG_INF = -1e30
BR_W = D_MODEL // 2
N_BRANCH = 4
HY_W = BR_W
HY_SHORT = 3
HY_EMB = 33
HY_FILT = 64
HY_FAST_PCT = 0.3
HY_SLOW_PCT = 1.5
HY_TARGET = 1e-2
SWA_HD = 128
SWA_HEADS = BR_W // SWA_HD
SWA_KV = SWA_HEADS // 4
SWA_WINDOW = 128
SWA_BLOCK = 128
LRU_W = BR_W
LRU_HEADS = 8
LRU_CONV = 4
LRU_PAD = 2
LRU_C = 8.0
RET_DK = 256
RET_DV = 256
RET_HEADS = BR_W // RET_DV
RET_CHUNK = 128
N_EXPERTS = 64
TOP_K = 8
EXP_FF = 512
SH_FF = 512
ROUTED_SCALE = 2.5
MOE_BLOCK = 128
SPLITS = (3 * HY_W, SWA_HEADS * SWA_HD, SWA_KV * SWA_HD, SWA_KV * SWA_HD, LRU_W, LRU_W,
          RET_HEADS * RET_DK, RET_HEADS * RET_DK, RET_HEADS * RET_DV, RET_HEADS * RET_DV,
          N_BRANCH * D_MODEL)
IN_COLS = sum(SPLITS)

V7X_VMEM_LIMIT_BYTES = 56 * 1024 * 1024


def _mm_kernel(x_ref, w_ref, o_ref, wb_ref):
    @pl.when(pl.program_id(1) == 0)
    def _():
        wb_ref[...] = w_ref[...].astype(jnp.bfloat16)

    o_ref[...] = jnp.dot(x_ref[...].astype(jnp.bfloat16), wb_ref[...],
                         preferred_element_type=jnp.float32).astype(o_ref.dtype)


def _pick(n, cands):
    for c in cands:
        if n % c == 0:
            return c
    return n


def _mm(x, w, out_dtype=jnp.float32):
    M, K = x.shape
    N = w.shape[1]
    tm = _pick(M, (512, 256, 128, 64, 32, 16, 8))
    tn = _pick(N, (1024, 512, 256, 128))
    return pl.pallas_call(
        _mm_kernel,
        grid=(N // tn, M // tm),
        in_specs=[pl.BlockSpec((tm, K), lambda n, m: (m, 0)),
                  pl.BlockSpec((K, tn), lambda n, m: (0, n))],
        out_specs=pl.BlockSpec((tm, tn), lambda n, m: (m, n)),
        out_shape=jax.ShapeDtypeStruct((M, N), out_dtype),
        scratch_shapes=[pltpu.VMEM((K, tn), jnp.bfloat16)],
        compiler_params=pltpu.CompilerParams(
            dimension_semantics=("arbitrary", "arbitrary"),
            vmem_limit_bytes=V7X_VMEM_LIMIT_BYTES),
        name="dense_mm",
    )(x, w)


def _mm3(x, w):
    lead = x.shape[:-1]
    return _mm(x.reshape(-1, x.shape[-1]), w).reshape(*lead, w.shape[1])


def _rmsnorm(x, g):
    x32 = x.astype(jnp.float32)
    y = x32 * lax.rsqrt(jnp.mean(x32 * x32, axis=-1, keepdims=True) + EPS)
    return y.astype(x.dtype) * g


def _modulate(x, shift, scale):
    return x * (1 + scale) + shift


def _split_cols(a, sizes):
    return jnp.split(a, np.cumsum(sizes)[:-1].tolist(), axis=-1)


def _dwconv(x, w, b, pad_left):
    K, C = w.shape
    y = lax.conv_general_dilated(x, w[:, None, :].astype(x.dtype), window_strides=(1,),
                                 padding=[(pad_left, K - 1 - pad_left)],
                                 dimension_numbers=('NWC', 'WIO', 'NWC'), feature_group_count=C)
    return y + b


def _rope_angles(pos, dim):
    inv = ROPE_THETA ** (-jnp.arange(0, dim, 2, dtype=jnp.float32) / dim)
    return pos.astype(jnp.float32)[:, None] * inv[None, :]


def _apply_rope(x, ang):
    cos = jnp.cos(ang)[None, :, None, :].astype(x.dtype)
    sin = jnp.sin(ang)[None, :, None, :].astype(x.dtype)
    x1, x2 = jnp.split(x, 2, axis=-1)
    return jnp.concatenate([x1 * cos - x2 * sin, x1 * sin + x2 * cos], axis=-1)


def _axial_angles(L, hd):
    rows = L // GRID_W
    row = jnp.repeat(jnp.arange(rows), GRID_W)
    col = jnp.tile(jnp.arange(GRID_W), rows)
    return jnp.concatenate([_rope_angles(row, hd // 2), _rope_angles(col, hd // 2)], axis=-1)


def _hyena_filters(L, w1, b1, w2, b2, w3, b3, freq, w_out):
    f32 = jnp.float32
    t = jnp.linspace(0.0, 1.0, L, dtype=f32)[:, None]
    bands = (HY_EMB - 1) // 2
    w = 2.0 * math.pi * jnp.arange(L, dtype=f32)[:, None] / L
    f = jnp.linspace(1e-4, bands - 1, bands, dtype=f32)[None, :]
    z = jnp.concatenate([t, jnp.cos(f * w), -jnp.sin(f * w)], axis=-1)
    h = jnp.sin(freq * (z @ w1 + b1))
    h = jnp.sin(freq * (h @ w2 + b2))
    h = jnp.sin(freq * (h @ w3 + b3))
    h = (h @ w_out).reshape(L, 2, 2, HY_W)
    deltas = jnp.abs(jnp.linspace(math.log(HY_TARGET) / HY_SLOW_PCT, math.log(HY_TARGET) / HY_FAST_PCT,
                                  2 * HY_W, dtype=f32)).reshape(2, 1, HY_W)
    h = h * jnp.exp(-t[:, :, None, None] * deltas[None])
    fwd = h[:, :, 0]
    bwd = h[:0:-1, :, 1]
    k = jnp.concatenate([fwd, jnp.zeros((1, 2, HY_W), f32), bwd], axis=0)
    return k / jnp.sum(jnp.abs(k), axis=0, keepdims=True)


def _fft_conv(z, kf, L):
    zf = jnp.fft.rfft(z.astype(jnp.float32), n=2 * L, axis=1)
    return jnp.fft.irfft(zf * kf[None], n=2 * L, axis=1)[:, :L].astype(z.dtype)


def _hyena_mixer(proj, conv_w, conv_b, kfilt, bias):
    L = proj.shape[1]
    uc = _dwconv(proj, conv_w, conv_b, (HY_SHORT - 1) // 2)
    x1, x2, v = jnp.split(uc, 3, axis=-1)
    kf = jnp.fft.rfft(kfilt, n=2 * L, axis=0)
    z = x1 * (_fft_conv(v, kf[:, 0], L) + bias[0] * v)
    return x2 * (_fft_conv(z, kf[:, 1], L) + bias[1] * z)


def _swa_latent(q, k, v, kc, vc, sink):
    B_, L, H, hd = q.shape
    KVH = k.shape[2]
    G = H // KVH
    W = SWA_BLOCK
    NB = L // W
    scale = hd ** -0.5
    qb = q.reshape(B_, NB, W, KVH, G, hd)

    def windows(a):
        ap = jnp.pad(a, ((0, 0), (W, W), (0, 0), (0, 0))).reshape(B_, NB + 2, W, KVH, hd)
        return jnp.concatenate([ap[:, :-2], ap[:, 1:-1], ap[:, 2:]], axis=2)

    kw, vw = windows(k), windows(v)
    s_loc = jnp.einsum('bnqkgd,bnskd->bnkgqs', qb, kw).astype(jnp.float32) * scale
    qpos = jnp.arange(NB)[:, None, None] * W + jnp.arange(W)[None, :, None]
    kpos = jnp.arange(NB)[:, None, None] * W - W + jnp.arange(3 * W)[None, None, :]
    valid = (jnp.abs(qpos - kpos) <= SWA_WINDOW) & (kpos >= 0) & (kpos < L)
    s_loc = jnp.where(valid[None, :, None, None], s_loc, NEG_INF)
    s_ctx = jnp.einsum('bnqkgd,bskd->bnkgqs', qb, kc).astype(jnp.float32) * scale
    snk = jnp.broadcast_to(sink.reshape(KVH, G)[None, None, :, :, None, None].astype(jnp.float32),
                           s_loc.shape[:-1] + (1,))
    p = jax.nn.softmax(jnp.concatenate([snk, s_loc, s_ctx], axis=-1), axis=-1)
    p_loc = p[..., 1:1 + 3 * W].astype(v.dtype)
    p_ctx = p[..., 1 + 3 * W:].astype(v.dtype)
    o = jnp.einsum('bnkgqs,bnskd->bnqkgd', p_loc, vw) + jnp.einsum('bnkgqs,bskd->bnqkgd', p_ctx, vc)
    return o.reshape(B_, L, H * hd)


def _attn_ctx(qc, kc, vc, sink):
    B_, Lc, H, hd = qc.shape
    KVH = kc.shape[2]
    G = H // KVH
    qg = qc.reshape(B_, Lc, KVH, G, hd)
    s = jnp.einsum('bqkgd,bskd->bkgqs', qg, kc).astype(jnp.float32) * hd ** -0.5
    snk = jnp.broadcast_to(sink.reshape(KVH, G)[None, :, :, None, None].astype(jnp.float32),
                           s.shape[:-1] + (1,))
    p = jax.nn.softmax(jnp.concatenate([snk, s], axis=-1), axis=-1)[..., 1:].astype(vc.dtype)
    return jnp.einsum('bkgqs,bskd->bqkgd', p, vc).reshape(B_, Lc, H * hd)


def _rglru_scan(xr, wa, ba, wx, bx, lam, h0, reverse):
    B_, L, C = xr.shape
    xh = xr.reshape(B_, L, LRU_HEADS, C // LRU_HEADS)
    r = jax.nn.sigmoid((jnp.einsum('blhi,hij->blhj', xh, wa).reshape(B_, L, C) + ba).astype(jnp.float32))
    i = jax.nn.sigmoid((jnp.einsum('blhi,hij->blhj', xh, wx).reshape(B_, L, C) + bx).astype(jnp.float32))
    log_a = -LRU_C * r * jax.nn.softplus(-lam.astype(jnp.float32))
    a = jnp.exp(log_a)
    b = jnp.sqrt(1.0 - jnp.exp(2.0 * log_a)) * (i * xr.astype(jnp.float32))

    def combine(e1, e2):
        a1, b1 = e1
        a2, b2 = e2
        return a1 * a2, a2 * b1 + b2

    A, Bs = lax.associative_scan(combine, (a, b), axis=1, reverse=reverse)
    return A * h0[:, None, :] + Bs


def _retention_chunks(q, k, v, log_g, s0, strict):
    f32 = jnp.float32
    B_, L, H, dk = q.shape
    dv = v.shape[-1]
    C = RET_CHUNK
    N = L // C
    qc = q.astype(f32).reshape(B_, N, C, H, dk)
    kc = k.astype(f32).reshape(B_, N, C, H, dk)
    vc = v.astype(f32).reshape(B_, N, C, H, dv)
    pos = jnp.arange(C, dtype=f32)
    diff = pos[:, None] - pos[None, :]
    keep = diff > 0 if strict else diff >= 0
    dmat = jnp.where(keep[None], jnp.exp(jnp.maximum(diff, 0.0)[None] * log_g[:, None, None]), 0.0)
    s = jnp.einsum('bnqhd,bnshd->bnhqs', qc, kc) * dmat[None, None]
    intra = jnp.einsum('bnhqs,bnshe->bnqhe', s, vc)
    zeta = jnp.exp((C - 1.0 - pos)[None, :] * log_g[:, None])
    upd = jnp.einsum('bnshd,bnshe,hs->nbhde', kc, vc, zeta)
    g_chunk = jnp.exp(C * log_g)[None, :, None, None]

    def step(S, u):
        return g_chunk * S + u, S

    s_last, s_prev = lax.scan(step, s0, upd)
    xi = jnp.exp((pos + 1.0)[None, :] * log_g[:, None])
    cross = jnp.einsum('bnqhd,nbhde,hq->bnqhe', qc, s_prev, xi)
    return (intra + cross).reshape(B_, L, H, dv), s_last


def _retention_bidir(q, k, v, log_g, s0f, s0b):
    of, sf = _retention_chunks(q, k, v, log_g[0], s0f, False)
    ob, sb = _retention_chunks(q[:, ::-1], k[:, ::-1], v[:, ::-1], log_g[1], s0b, True)
    return of + ob[:, ::-1], sf, sb


def _retention_out(o, gate):
    B_, L = o.shape[:2]
    n = o * lax.rsqrt(jnp.mean(o * o, axis=-1, keepdims=True) + EPS)
    return n.reshape(B_, L, RET_HEADS * RET_DV).astype(gate.dtype) * jax.nn.silu(gate)


def _merge_branches(ys, gate_logits, w_branch, w_out):
    gates = jax.nn.sigmoid(gate_logits.astype(jnp.float32)).astype(gate_logits.dtype)
    gates = jnp.split(gates, N_BRANCH, axis=-1)
    merged = gates[0] * _mm3(ys[0], w_branch[0])
    for b in range(1, N_BRANCH):
        merged = merged + gates[b] * _mm3(ys[b], w_branch[b])
    return _mm3(merged, w_out)


def _moe_ffn(xf, router_w, router_bias, wg, wu, wd, sg, su, sd):
    T, D = xf.shape
    E = wg.shape[0]
    dt = xf.dtype
    scores = jax.nn.sigmoid((xf @ router_w).astype(jnp.float32))
    _, idx = lax.top_k(scores + router_bias.astype(jnp.float32), TOP_K)
    w_sel = jnp.take_along_axis(scores, idx, axis=-1)
    w_sel = ROUTED_SCALE * w_sel / jnp.sum(w_sel, axis=-1, keepdims=True)
    A = T * TOP_K
    flat_e = idx.reshape(A)
    order = jnp.argsort(flat_e)
    se = flat_e[order]
    st = (jnp.arange(A, dtype=jnp.int32) // TOP_K)[order]
    sw = w_sel.reshape(A)[order]
    counts = jnp.bincount(flat_e, length=E)
    padded = (counts + MOE_BLOCK - 1) // MOE_BLOCK * MOE_BLOCK
    start = jnp.cumsum(counts) - counts
    pend = jnp.cumsum(padded)
    pstart = pend - padded
    dest = pstart[se] + jnp.arange(A, dtype=jnp.int32) - start[se]
    n_blocks = (A + E * (MOE_BLOCK - 1) + MOE_BLOCK - 1) // MOE_BLOCK
    P = n_blocks * MOE_BLOCK
    slot_tok = jnp.full((P,), T, jnp.int32).at[dest].set(st)
    slot_w = jnp.zeros((P,), jnp.float32).at[dest].set(sw)
    block_e = jnp.minimum(jnp.searchsorted(pend, jnp.arange(n_blocks, dtype=jnp.int32) * MOE_BLOCK,
                                           side='right'), E - 1)
    x_pad = jnp.concatenate([xf, jnp.zeros((1, D), dt)], axis=0)

    def block(acc, blk):
        tok, wts, e = blk
        xb = x_pad[tok]
        hb = jax.nn.silu(xb @ wg[e]) * (xb @ wu[e])
        return acc.at[tok].add((hb @ wd[e]) * wts.astype(dt)[:, None]), None

    acc, _ = lax.scan(block, jnp.zeros((T + 1, D), dt),
                      (slot_tok.reshape(n_blocks, MOE_BLOCK), slot_w.reshape(n_blocks, MOE_BLOCK), block_e))
    shared = _mm(jax.nn.silu(_mm(xf, sg)) * _mm(xf, su), sd)
    return acc[:T] + shared


def _hybrid_layer(x, ctx, mod_x, mod_c, p, update_ctx):
    B_, L, D = x.shape
    Lc = ctx.shape[1]
    dt = x.dtype
    f32 = jnp.float32
    sh1, sc1, g1, sh2, sc2, g2 = mod_x
    csh1, csc1, cg1, csh2, csc2, cg2 = mod_c
    u = _modulate(_rmsnorm(x, p['norm_mix']), sh1, sc1)
    uc = _modulate(_rmsnorm(ctx, p['norm_mix']), csh1, csc1)
    hy, sq, sk, sv, lgt, lrc, rq, rk, rv, rgt, gl = _split_cols(_mm3(u, p['w_in']), SPLITS)
    hy_c, sq_c, sk_c, sv_c, lgt_c, lrc_c, rq_c, rk_c, rv_c, rgt_c, gl_c = _split_cols(_mm3(uc, p['w_in']), SPLITS)

    ya = _hyena_mixer(hy, p['hy_conv_w'], p['hy_conv_b'], _hyena_filters(L, *p['hy_filt']), p['hy_bias'])

    ang = _axial_angles(L, SWA_HD)
    q = _apply_rope(sq.reshape(B_, L, SWA_HEADS, SWA_HD), ang)
    k = _apply_rope(sk.reshape(B_, L, SWA_KV, SWA_HD), ang)
    v = sv.reshape(B_, L, SWA_KV, SWA_HD)
    k_c = sk_c.reshape(B_, Lc, SWA_KV, SWA_HD)
    v_c = sv_c.reshape(B_, Lc, SWA_KV, SWA_HD)
    yb = _swa_latent(q, k, v, k_c, v_c, p['swa_sink'])

    xr = _dwconv(lrc, p['lru_conv_w'], p['lru_conv_b'], LRU_PAD)
    xr_c = _dwconv(lrc_c, p['lru_conv_w'], p['lru_conv_b'], LRU_PAD)
    h0 = jnp.zeros((B_, LRU_W), f32)
    hf_c = _rglru_scan(xr_c, *p['lru_f'], h0, False)
    hb_c = _rglru_scan(xr_c, *p['lru_b'], h0, True)
    hf = _rglru_scan(xr, *p['lru_f'], hf_c[:, -1], False)
    hb = _rglru_scan(xr, *p['lru_b'], hb_c[:, 0], True)
    yc = jax.nn.gelu(lgt) * (hf + hb).astype(dt)

    log_g = -jnp.abs(p['ret_log_decay'].astype(f32))
    ang_r = _rope_angles(jnp.arange(L), RET_DK)
    kscale = RET_DK ** -0.5
    rq_l = _apply_rope(rq.reshape(B_, L, RET_HEADS, RET_DK), ang_r)
    rk_l = _apply_rope(rk.reshape(B_, L, RET_HEADS, RET_DK), ang_r) * kscale
    rv_l = rv.reshape(B_, L, RET_HEADS, RET_DV)
    s0 = jnp.zeros((B_, RET_HEADS, RET_DK, RET_DV), f32)
    o_c, sf_c, sb_c = _retention_bidir(rq_c.reshape(B_, Lc, RET_HEADS, RET_DK),
                                       rk_c.reshape(B_, Lc, RET_HEADS, RET_DK) * kscale,
                                       rv_c.reshape(B_, Lc, RET_HEADS, RET_DV), log_g, s0, s0)
    o_l, _, _ = _retention_bidir(rq_l, rk_l, rv_l, log_g, sf_c, sb_c)
    yd = _retention_out(o_l, rgt)

    x = x + g1 * _merge_branches((ya, yb, yc, yd), gl, p['w_branch'], p['w_out'])
    h = _modulate(_rmsnorm(x, p['norm_ffn']), sh2, sc2).reshape(B_ * L, D)

    if update_ctx:
        ya_c = _hyena_mixer(hy_c, p['hy_conv_w'], p['hy_conv_b'], _hyena_filters(Lc, *p['hy_filt']), p['hy_bias'])
        yb_c = _attn_ctx(sq_c.reshape(B_, Lc, SWA_HEADS, SWA_HD), k_c, v_c, p['swa_sink'])
        yc_c = jax.nn.gelu(lgt_c) * (hf_c + hb_c).astype(dt)
        yd_c = _retention_out(o_c, rgt_c)
        ctx = ctx + cg1 * _merge_branches((ya_c, yb_c, yc_c, yd_c), gl_c, p['w_branch'], p['w_out'])
        hc = _modulate(_rmsnorm(ctx, p['norm_ffn']), csh2, csc2).reshape(B_ * Lc, D)
        y = _moe_ffn(jnp.concatenate([h, hc], axis=0), *p['moe'])
        ctx = ctx + cg2 * y[B_ * L:].reshape(B_, Lc, D)
    else:
        y = _moe_ffn(h, *p['moe'])
    x = x + g2 * y[:B_ * L].reshape(B_, L, D)
    return x, ctx


def kernel(x, c, ctx, c_ctx, ada_w, ada_b, norm_mix, norm_ffn, w_in, hy_conv_w, hy_conv_b,
           hy_filt_w1, hy_filt_b1, hy_filt_w2, hy_filt_b2, hy_filt_w3, hy_filt_b3, hy_filt_freq,
           hy_filt_out, hy_bias, swa_sink, lru_conv_w, lru_conv_b, lru_wa, lru_ba, lru_wx, lru_bx,
           lru_lambda, ret_log_decay, w_branch, w_out, router_w, router_bias, exp_w_gate, exp_w_up,
           exp_w_down, sh_w_gate, sh_w_up, sh_w_down, final_norm):
    s_c = jax.nn.silu(c)
    s_cc = jax.nn.silu(c_ctx)
    for l in range(DEPTH):
        p = {
            'norm_mix': norm_mix[l], 'norm_ffn': norm_ffn[l], 'w_in': w_in[l],
            'hy_conv_w': hy_conv_w[l], 'hy_conv_b': hy_conv_b[l],
            'hy_filt': (hy_filt_w1[l], hy_filt_b1[l], hy_filt_w2[l], hy_filt_b2[l], hy_filt_w3[l],
                        hy_filt_b3[l], hy_filt_freq[l], hy_filt_out[l]),
            'hy_bias': hy_bias[l], 'swa_sink': swa_sink[l],
            'lru_conv_w': lru_conv_w[l], 'lru_conv_b': lru_conv_b[l],
            'lru_f': (lru_wa[l][0], lru_ba[l][0], lru_wx[l][0], lru_bx[l][0], lru_lambda[l][0]),
            'lru_b': (lru_wa[l][1], lru_ba[l][1], lru_wx[l][1], lru_bx[l][1], lru_lambda[l][1]),
            'ret_log_decay': ret_log_decay[l], 'w_branch': w_branch[l], 'w_out': w_out[l],
            'moe': (router_w[l], router_bias[l], exp_w_gate[l], exp_w_up[l], exp_w_down[l],
                    sh_w_gate[l], sh_w_up[l], sh_w_down[l]),
        }
        mod_x = jnp.split((s_c @ ada_w[l] + ada_b[l])[:, None, :], 6, axis=-1)
        mod_c = jnp.split(s_cc @ ada_w[l] + ada_b[l], 6, axis=-1)
        x, ctx = _hybrid_layer(x, ctx, mod_x, mod_c, p, l < DEPTH - 1)
    return _rmsnorm(x, final_norm)
```

```python
import functools
import math

import jax
import jax.numpy as jnp
import numpy as np
from jax import lax
from jax.experimental import pallas as pl
from jax.experimental.pallas import tpu as pltpu

D_MODEL = 2048
BATCH = 2
SEQ = 4096
DEPTH = 2
GRID_W = 64
CTX_LEN = 256
EPS = 1e-6
ROPE_THETA = 10000.0
NEG_INF = -1e30
BR_W = D_MODEL // 2
N_BRANCH = 4
HY_W = BR_W
HY_SHORT = 3
HY_EMB = 33
HY_FILT = 64
HY_FAST_PCT = 0.3
HY_SLOW_PCT = 1.5
HY_TARGET = 1e-2
SWA_HD = 128
SWA_HEADS = BR_W // SWA_HD
SWA_KV = SWA_HEADS // 4
SWA_WINDOW = 128
SWA_BLOCK = 128
LRU_W = BR_W
LRU_HEADS = 8
LRU_CONV = 4
LRU_PAD = 2
LRU_C = 8.0
RET_DK = 256
RET_DV = 256
RET_HEADS = BR_W // RET_DV
RET_CHUNK = 128
N_EXPERTS = 64
TOP_K = 8
EXP_FF = 512
SH_FF = 512
ROUTED_SCALE = 2.5
MOE_BLOCK = 128
SPLITS = (3 * HY_W, SWA_HEADS * SWA_HD, SWA_KV * SWA_HD, SWA_KV * SWA_HD, LRU_W, LRU_W,
          RET_HEADS * RET_DK, RET_HEADS * RET_DK, RET_HEADS * RET_DV, RET_HEADS * RET_DV,
          N_BRANCH * D_MODEL)
IN_COLS = sum(SPLITS)

V7X_VMEM_LIMIT_BYTES = 56 * 1024 * 1024


def _mm_kernel(x_ref, w_ref, o_ref, wb_ref):
    @pl.when(pl.program_id(1) == 0)
    def _():
        wb_ref[...] = w_ref[...].astype(jnp.bfloat16)

    o_ref[...] = jnp.dot(x_ref[...].astype(jnp.bfloat16), wb_ref[...],
                         preferred_element_type=jnp.float32).astype(o_ref.dtype)


def _pick(n, cands):
    for c in cands:
        if n % c == 0:
            return c
    return n


def _mm(x, w, out_dtype=jnp.float32):
    M, K = x.shape
    N = w.shape[1]
    tm = _pick(M, (512, 256, 128, 64, 32, 16, 8))
    tn = _pick(N, (1024, 512, 256, 128))
    return pl.pallas_call(
        _mm_kernel,
        grid=(N // tn, M // tm),
        in_specs=[pl.BlockSpec((tm, K), lambda n, m: (m, 0)),
                  pl.BlockSpec((K, tn), lambda n, m: (0, n))],
        out_specs=pl.BlockSpec((tm, tn), lambda n, m: (m, n)),
        out_shape=jax.ShapeDtypeStruct((M, N), out_dtype),
        scratch_shapes=[pltpu.VMEM((K, tn), jnp.bfloat16)],
        compiler_params=pltpu.CompilerParams(
            dimension_semantics=("arbitrary", "arbitrary"),
            vmem_limit_bytes=V7X_VMEM_LIMIT_BYTES),
        name="dense_mm",
    )(x, w)


def _mm3(x, w):
    lead = x.shape[:-1]
    return _mm(x.reshape(-1, x.shape[-1]), w).reshape(*lead, w.shape[1])


def _rmsnorm(x, g):
    x32 = x.astype(jnp.float32)
    y = x32 * lax.rsqrt(jnp.mean(x32 * x32, axis=-1, keepdims=True) + EPS)
    return y.astype(x.dtype) * g


def _modulate(x, shift, scale):
    return x * (1 + scale) + shift


def _split_cols(a, sizes):
    return jnp.split(a, np.cumsum(sizes)[:-1].tolist(), axis=-1)


def _dwconv(x, w, b, pad_left):
    K, C = w.shape
    y = lax.conv_general_dilated(x, w[:, None, :].astype(x.dtype), window_strides=(1,),
                                 padding=[(pad_left, K - 1 - pad_left)],
                                 dimension_numbers=('NWC', 'WIO', 'NWC'), feature_group_count=C)
    return y + b


def _rope_angles(pos, dim):
    inv = ROPE_THETA ** (-jnp.arange(0, dim, 2, dtype=jnp.float32) / dim)
    return pos.astype(jnp.float32)[:, None] * inv[None, :]


def _apply_rope(x, ang):
    cos = jnp.cos(ang)[None, :, None, :].astype(x.dtype)
    sin = jnp.sin(ang)[None, :, None, :].astype(x.dtype)
    x1, x2 = jnp.split(x, 2, axis=-1)
    return jnp.concatenate([x1 * cos - x2 * sin, x1 * sin + x2 * cos], axis=-1)


def _axial_angles(L, hd):
    rows = L // GRID_W
    row = jnp.repeat(jnp.arange(rows), GRID_W)
    col = jnp.tile(jnp.arange(GRID_W), rows)
    return jnp.concatenate([_rope_angles(row, hd // 2), _rope_angles(col, hd // 2)], axis=-1)


def _hyena_filters(L, w1, b1, w2, b2, w3, b3, freq, w_out):
    f32 = jnp.float32
    t = jnp.linspace(0.0, 1.0, L, dtype=f32)[:, None]
    bands = (HY_EMB - 1) // 2
    w = 2.0 * math.pi * jnp.arange(L, dtype=f32)[:, None] / L
    f = jnp.linspace(1e-4, bands - 1, bands, dtype=f32)[None, :]
    z = jnp.concatenate([t, jnp.cos(f * w), -jnp.sin(f * w)], axis=-1)
    h = jnp.sin(freq * (z @ w1 + b1))
    h = jnp.sin(freq * (h @ w2 + b2))
    h = jnp.sin(freq * (h @ w3 + b3))
    h = (h @ w_out).reshape(L, 2, 2, HY_W)
    deltas = jnp.abs(jnp.linspace(math.log(HY_TARGET) / HY_SLOW_PCT, math.log(HY_TARGET) / HY_FAST_PCT,
                                  2 * HY_W, dtype=f32)).reshape(2, 1, HY_W)
    h = h * jnp.exp(-t[:, :, None, None] * deltas[None])
    fwd = h[:, :, 0]
    bwd = h[:0:-1, :, 1]
    k = jnp.concatenate([fwd, jnp.zeros((1, 2, HY_W), f32), bwd], axis=0)
    return k / jnp.sum(jnp.abs(k), axis=0, keepdims=True)


def _fft_conv(z, kf, L):
    zf = jnp.fft.rfft(z.astype(jnp.float32), n=2 * L, axis=1)
    return jnp.fft.irfft(zf * kf[None], n=2 * L, axis=1)[:, :L].astype(z.dtype)


def _hyena_mixer(proj, conv_w, conv_b, kfilt, bias):
    L = proj.shape[1]
    uc = _dwconv(proj, conv_w, conv_b, (HY_SHORT - 1) // 2)
    x1, x2, v = jnp.split(uc, 3, axis=-1)
    kf = jnp.fft.rfft(kfilt, n=2 * L, axis=0)
    z = x1 * (_fft_conv(v, kf[:, 0], L) + bias[0] * v)
    return x2 * (_fft_conv(z, kf[:, 1], L) + bias[1] * z)


def _swa_latent(q, k, v, kc, vc, sink):
    B_, L, H, hd = q.shape
    KVH = k.shape[2]
    G = H // KVH
    W = SWA_BLOCK
    NB = L // W
    scale = hd ** -0.5
    qb = q.reshape(B_, NB, W, KVH, G, hd)

    def windows(a):
        ap = jnp.pad(a, ((0, 0), (W, W), (0, 0), (0, 0))).reshape(B_, NB + 2, W, KVH, hd)
        return jnp.concatenate([ap[:, :-2], ap[:, 1:-1], ap[:, 2:]], axis=2)

    kw, vw = windows(k), windows(v)
    s_loc = jnp.einsum('bnqkgd,bnskd->bnkgqs', qb, kw).astype(jnp.float32) * scale
    qpos = jnp.arange(NB)[:, None, None] * W + jnp.arange(W)[None, :, None]
    kpos = jnp.arange(NB)[:, None, None] * W - W + jnp.arange(3 * W)[None, None, :]
    valid = (jnp.abs(qpos - kpos) <= SWA_WINDOW) & (kpos >= 0) & (kpos < L)
    s_loc = jnp.where(valid[None, :, None, None], s_loc, NEG_INF)
    s_ctx = jnp.einsum('bnqkgd,bskd->bnkgqs', qb, kc).astype(jnp.float32) * scale
    snk = jnp.broadcast_to(sink.reshape(KVH, G)[None, None, :, :, None, None].astype(jnp.float32),
                           s_loc.shape[:-1] + (1,))
    p = jax.nn.softmax(jnp.concatenate([snk, s_loc, s_ctx], axis=-1), axis=-1)
    p_loc = p[..., 1:1 + 3 * W].astype(v.dtype)
    p_ctx = p[..., 1 + 3 * W:].astype(v.dtype)
    o = jnp.einsum('bnkgqs,bnskd->bnqkgd', p_loc, vw) + jnp.einsum('bnkgqs,bskd->bnqkgd', p_ctx, vc)
    return o.reshape(B_, L, H * hd)


def _attn_ctx(qc, kc, vc, sink):
    B_, Lc, H, hd = qc.shape
    KVH = kc.shape[2]
    G = H // KVH
    qg = qc.reshape(B_, Lc, KVH, G, hd)
    s = jnp.einsum('bqkgd,bskd->bkgqs', qg, kc).astype(jnp.float32) * hd ** -0.5
    snk = jnp.broadcast_to(sink.reshape(KVH, G)[None, :, :, None, None].astype(jnp.float32),
                           s.shape[:-1] + (1,))
    p = jax.nn.softmax(jnp.concatenate([snk, s], axis=-1), axis=-1)[..., 1:].astype(vc.dtype)
    return jnp.einsum('bkgqs,bskd->bqkgd', p, vc).reshape(B_, Lc, H * hd)


def _rglru_scan(xr, wa, ba, wx, bx, lam, h0, reverse):
    B_, L, C = xr.shape
    xh = xr.reshape(B_, L, LRU_HEADS, C // LRU_HEADS)
    r = jax.nn.sigmoid((jnp.einsum('blhi,hij->blhj', xh, wa).reshape(B_, L, C) + ba).astype(jnp.float32))
    i = jax.nn.sigmoid((jnp.einsum('blhi,hij->blhj', xh, wx).reshape(B_, L, C) + bx).astype(jnp.float32))
    log_a = -LRU_C * r * jax.nn.softplus(-lam.astype(jnp.float32))
    a = jnp.exp(log_a)
    b = jnp.sqrt(1.0 - jnp.exp(2.0 * log_a)) * (i * xr.astype(jnp.float32))

    def combine(e1, e2):
        a1, b1 = e1
        a2, b2 = e2
        return a1 * a2, a2 * b1 + b2

    A, Bs = lax.associative_scan(combine, (a, b), axis=1, reverse=reverse)
    return A * h0[:, None, :] + Bs


def _retention_chunks(q, k, v, log_g, s0, strict):
    f32 = jnp.float32
    B_, L, H, dk = q.shape
    dv = v.shape[-1]
    C = RET_CHUNK
    N = L // C
    qc = q.astype(f32).reshape(B_, N, C, H, dk)
    kc = k.astype(f32).reshape(B_, N, C, H, dk)
    vc = v.astype(f32).reshape(B_, N, C, H, dv)
    pos = jnp.arange(C, dtype=f32)
    diff = pos[:, None] - pos[None, :]
    keep = diff > 0 if strict else diff >= 0
    dmat = jnp.where(keep[None], jnp.exp(jnp.maximum(diff, 0.0)[None] * log_g[:, None, None]), 0.0)
    s = jnp.einsum('bnqhd,bnshd->bnhqs', qc, kc) * dmat[None, None]
    intra = jnp.einsum('bnhqs,bnshe->bnqhe', s, vc)
    zeta = jnp.exp((C - 1.0 - pos)[None, :] * log_g[:, None])
    upd = jnp.einsum('bnshd,bnshe,hs->nbhde', kc, vc, zeta)
    g_chunk = jnp.exp(C * log_g)[None, :, None, None]

    def step(S, u):
        return g_chunk * S + u, S

    s_last, s_prev = lax.scan(step, s0, upd)
    xi = jnp.exp((pos + 1.0)[None, :] * log_g[:, None])
    cross = jnp.einsum('bnqhd,nbhde,hq->bnqhe', qc, s_prev, xi)
    return (intra + cross).reshape(B_, L, H, dv), s_last


def _retention_bidir(q, k, v, log_g, s0f, s0b):
    of, sf = _retention_chunks(q, k, v, log_g[0], s0f, False)
    ob, sb = _retention_chunks(q[:, ::-1], k[:, ::-1], v[:, ::-1], log_g[1], s0b, True)
    return of + ob[:, ::-1], sf, sb


def _retention_out(o, gate):
    B_, L = o.shape[:2]
    n = o * lax.rsqrt(jnp.mean(o * o, axis=-1, keepdims=True) + EPS)
    return n.reshape(B_, L, RET_HEADS * RET_DV).astype(gate.dtype) * jax.nn.silu(gate)


def _merge_branches(ys, gate_logits, w_branch, w_out):
    gates = jax.nn.sigmoid(gate_logits.astype(jnp.float32)).astype(gate_logits.dtype)
    gates = jnp.split(gates, N_BRANCH, axis=-1)
    merged = gates[0] * _mm3(ys[0], w_branch[0])
    for b in range(1, N_BRANCH):
        merged = merged + gates[b] * _mm3(ys[b], w_branch[b])
    return _mm3(merged, w_out)


MOE_BM = 256


def _pack_bf16_pairs(x):
    T, D = x.shape
    b = lax.bitcast_convert_type(x.astype(jnp.bfloat16), jnp.uint16).astype(jnp.uint32)
    return b[:, :D // 2] | (b[:, D // 2:] << 16)


def _unpack_lo(p):
    return pltpu.bitcast(p << 16, jnp.float32).astype(jnp.bfloat16)


def _unpack_hi(p):
    return pltpu.bitcast(p & jnp.uint32(0xFFFF0000), jnp.float32).astype(jnp.bfloat16)


def _moe_up_kernel(bi_ref, be_ref, nv_ref, tok_ref, xp_ref, wg_ref, wu_ref, h_ref, buf_ref):
    i = pl.program_id(0)
    nv = nv_ref[i]
    half = xp_ref.shape[1]

    @pl.when(i == 0)
    def _():
        buf_ref[...] = jnp.zeros_like(buf_ref)

    @pl.when(nv > 0)
    def _():
        def row(r, c):
            t = tok_ref[0, r]
            buf_ref[pl.ds(r, 1), :] = xp_ref[pl.ds(t, 1), :]
            return c
        lax.fori_loop(0, nv, row, 0)
        p = buf_ref[...]
        lo = _unpack_lo(p)
        hi = _unpack_hi(p)
        g = (jnp.dot(lo, wg_ref[:half, :], preferred_element_type=jnp.float32)
             + jnp.dot(hi, wg_ref[half:, :], preferred_element_type=jnp.float32))
        u = (jnp.dot(lo, wu_ref[:half, :], preferred_element_type=jnp.float32)
             + jnp.dot(hi, wu_ref[half:, :], preferred_element_type=jnp.float32))
        h_ref[...] = (g * jax.nn.sigmoid(g) * u).astype(h_ref.dtype)


def _moe_up(xp, tok3, blk_idx, blk_e, blk_nv, wg, wu):
    T, half = xp.shape
    NB, _, BM = tok3.shape
    E, D, F = wg.shape
    grid_spec = pltpu.PrefetchScalarGridSpec(
        num_scalar_prefetch=3, grid=(NB,),
        in_specs=[
            pl.BlockSpec((None, 1, BM), lambda i, bi, be, nv: (bi[i], 0, 0), memory_space=pltpu.SMEM),
            pl.BlockSpec((T, half), lambda i, bi, be, nv: (0, 0), pipeline_mode=pl.Buffered(1)),
            pl.BlockSpec((None, D, F), lambda i, bi, be, nv: (be[i], 0, 0)),
            pl.BlockSpec((None, D, F), lambda i, bi, be, nv: (be[i], 0, 0)),
        ],
        out_specs=pl.BlockSpec((BM, F), lambda i, bi, be, nv: (bi[i], 0)),
        scratch_shapes=[pltpu.VMEM((BM, half), jnp.uint32)])
    return pl.pallas_call(
        _moe_up_kernel, grid_spec=grid_spec,
        out_shape=jax.ShapeDtypeStruct((NB * BM, F), jnp.bfloat16),
        compiler_params=pltpu.CompilerParams(dimension_semantics=("arbitrary",),
                                             vmem_limit_bytes=V7X_VMEM_LIMIT_BYTES),
        name="moe_up")(blk_idx, blk_e, blk_nv, tok3, xp, wg, wu)


def _moe_down_kernel(bi_ref, be_ref, nv_ref, tok_ref, h_ref, w_ref, wd_ref, acc_ref, ybuf_ref):
    i = pl.program_id(1)
    nv = nv_ref[i]

    @pl.when(i == 0)
    def _():
        acc_ref[...] = jnp.zeros_like(acc_ref)

    @pl.when(nv > 0)
    def _():
        y = jnp.dot(h_ref[...], wd_ref[...], preferred_element_type=jnp.float32)
        ybuf_ref[...] = y * w_ref[...]

        def row(r, c):
            t = tok_ref[0, r]
            acc_ref[pl.ds(t, 1), :] = acc_ref[pl.ds(t, 1), :] + ybuf_ref[pl.ds(r, 1), :]
            return c
        lax.fori_loop(0, nv, row, 0)


def _moe_down(h, tok3, w3, blk_idx, blk_e, blk_nv, wd, T):
    NB, _, BM = tok3.shape
    E, F, D = wd.shape
    half = D // 2
    grid_spec = pltpu.PrefetchScalarGridSpec(
        num_scalar_prefetch=3, grid=(2, NB),
        in_specs=[
            pl.BlockSpec((None, 1, BM), lambda c, i, bi, be, nv: (bi[i], 0, 0), memory_space=pltpu.SMEM),
            pl.BlockSpec((BM, F), lambda c, i, bi, be, nv: (bi[i], 0)),
            pl.BlockSpec((None, BM, 1), lambda c, i, bi, be, nv: (bi[i], 0, 0)),
            pl.BlockSpec((None, F, half), lambda c, i, bi, be, nv: (be[i], 0, c)),
        ],
        out_specs=pl.BlockSpec((T, half), lambda c, i, bi, be, nv: (0, c), pipeline_mode=pl.Buffered(1)),
        scratch_shapes=[pltpu.VMEM((BM, half), jnp.float32)])
    return pl.pallas_call(
        _moe_down_kernel, grid_spec=grid_spec,
        out_shape=jax.ShapeDtypeStruct((T, D), jnp.float32),
        compiler_params=pltpu.CompilerParams(dimension_semantics=("arbitrary", "arbitrary"),
                                             vmem_limit_bytes=V7X_VMEM_LIMIT_BYTES),
        name="moe_down")(blk_idx, blk_e, blk_nv, tok3, h, w3, wd)


def _moe_plan(idx, w_sel, T, E, BM):
    A = T * TOP_K
    NB = (A + E * (BM - 1) + BM - 1) // BM
    onehot = (idx[:, :, None] == jnp.arange(E, dtype=idx.dtype)[None, None, :]).astype(jnp.int32).sum(axis=1)
    csum = jnp.cumsum(onehot, axis=0)
    counts = csum[-1]
    rank = jnp.take_along_axis(csum - onehot, idx, axis=1)
    nb_e = (counts + BM - 1) // BM
    bend = jnp.cumsum(nb_e)
    bstart = bend - nb_e
    slot = (bstart[idx] * BM + rank).reshape(A)
    tok = jnp.repeat(jnp.arange(T, dtype=jnp.int32), TOP_K)
    slot_tok = jnp.zeros((NB * BM,), jnp.int32).at[slot].set(tok)
    slot_w = jnp.zeros((NB * BM,), jnp.float32).at[slot].set(w_sel.reshape(A))
    nvb = bend[-1]
    bids = jnp.arange(NB, dtype=jnp.int32)
    blk_idx = jnp.minimum(bids, nvb - 1).astype(jnp.int32)
    blk_e = jnp.minimum(jnp.searchsorted(bend, blk_idx, side='right'), E - 1).astype(jnp.int32)
    blk_nv = jnp.where(bids < nvb, jnp.clip(counts[blk_e] - (blk_idx - bstart[blk_e]) * BM, 0, BM), 0).astype(jnp.int32)
    return slot_tok.reshape(NB, 1, BM), slot_w.reshape(NB, BM, 1), blk_idx, blk_e, blk_nv


def _moe_routed(xf, scores, router_bias, wg, wu, wd):
    T, D = xf.shape
    E = wg.shape[0]
    _, idx = lax.top_k(scores + router_bias.astype(jnp.float32), TOP_K)
    w_sel = jnp.take_along_axis(scores, idx, axis=-1)
    w_sel = ROUTED_SCALE * w_sel / jnp.sum(w_sel, axis=-1, keepdims=True)
    tok3, w3, blk_idx, blk_e, blk_nv = _moe_plan(idx, w_sel, T, E, MOE_BM)
    xp = _pack_bf16_pairs(xf)
    bf = jnp.bfloat16
    h = _moe_up(xp, tok3, blk_idx, blk_e, blk_nv, wg.astype(bf), wu.astype(bf))
    return _moe_down(h, tok3, w3, blk_idx, blk_e, blk_nv, wd.astype(bf), T)


def _moe_ffn(xf, router_w, router_bias, wg, wu, wd, sg, su, sd):
    scores = jax.nn.sigmoid(_mm(xf, router_w))
    routed = _moe_routed(xf, scores, router_bias, wg, wu, wd)
    shared = _mm(jax.nn.silu(_mm(xf, sg)) * _mm(xf, su), sd)
    return routed + shared


def _hybrid_layer(x, ctx, mod_x, mod_c, p, update_ctx):
    B_, L, D = x.shape
    Lc = ctx.shape[1]
    dt = x.dtype
    f32 = jnp.float32
    sh1, sc1, g1, sh2, sc2, g2 = mod_x
    csh1, csc1, cg1, csh2, csc2, cg2 = mod_c
    u = _modulate(_rmsnorm(x, p['norm_mix']), sh1, sc1)
    uc = _modulate(_rmsnorm(ctx, p['norm_mix']), csh1, csc1)
    hy, sq, sk, sv, lgt, lrc, rq, rk, rv, rgt, gl = _split_cols(_mm3(u, p['w_in']), SPLITS)
    hy_c, sq_c, sk_c, sv_c, lgt_c, lrc_c, rq_c, rk_c, rv_c, rgt_c, gl_c = _split_cols(_mm3(uc, p['w_in']), SPLITS)

    ya = _hyena_mixer(hy, p['hy_conv_w'], p['hy_conv_b'], _hyena_filters(L, *p['hy_filt']), p['hy_bias'])

    ang = _axial_angles(L, SWA_HD)
    q = _apply_rope(sq.reshape(B_, L, SWA_HEADS, SWA_HD), ang)
    k = _apply_rope(sk.reshape(B_, L, SWA_KV, SWA_HD), ang)
    v = sv.reshape(B_, L, SWA_KV, SWA_HD)
    k_c = sk_c.reshape(B_, Lc, SWA_KV, SWA_HD)
    v_c = sv_c.reshape(B_, Lc, SWA_KV, SWA_HD)
    yb = _swa_latent(q, k, v, k_c, v_c, p['swa_sink'])

    xr = _dwconv(lrc, p['lru_conv_w'], p['lru_conv_b'], LRU_PAD)
    xr_c = _dwconv(lrc_c, p['lru_conv_w'], p['lru_conv_b'], LRU_PAD)
    h0 = jnp.zeros((B_, LRU_W), f32)
    hf_c = _rglru_scan(xr_c, *p['lru_f'], h0, False)
    hb_c = _rglru_scan(xr_c, *p['lru_b'], h0, True)
    hf = _rglru_scan(xr, *p['lru_f'], hf_c[:, -1], False)
    hb = _rglru_scan(xr, *p['lru_b'], hb_c[:, 0], True)
    yc = jax.nn.gelu(lgt) * (hf + hb).astype(dt)

    log_g = -jnp.abs(p['ret_log_decay'].astype(f32))
    ang_r = _rope_angles(jnp.arange(L), RET_DK)
    kscale = RET_DK ** -0.5
    rq_l = _apply_rope(rq.reshape(B_, L, RET_HEADS, RET_DK), ang_r)
    rk_l = _apply_rope(rk.reshape(B_, L, RET_HEADS, RET_DK), ang_r) * kscale
    rv_l = rv.reshape(B_, L, RET_HEADS, RET_DV)
    s0 = jnp.zeros((B_, RET_HEADS, RET_DK, RET_DV), f32)
    o_c, sf_c, sb_c = _retention_bidir(rq_c.reshape(B_, Lc, RET_HEADS, RET_DK),
                                       rk_c.reshape(B_, Lc, RET_HEADS, RET_DK) * kscale,
                                       rv_c.reshape(B_, Lc, RET_HEADS, RET_DV), log_g, s0, s0)
    o_l, _, _ = _retention_bidir(rq_l, rk_l, rv_l, log_g, sf_c, sb_c)
    yd = _retention_out(o_l, rgt)

    x = x + g1 * _merge_branches((ya, yb, yc, yd), gl, p['w_branch'], p['w_out'])
    h = _modulate(_rmsnorm(x, p['norm_ffn']), sh2, sc2).reshape(B_ * L, D)

    if update_ctx:
        ya_c = _hyena_mixer(hy_c, p['hy_conv_w'], p['hy_conv_b'], _hyena_filters(Lc, *p['hy_filt']), p['hy_bias'])
        yb_c = _attn_ctx(sq_c.reshape(B_, Lc, SWA_HEADS, SWA_HD), k_c, v_c, p['swa_sink'])
        yc_c = jax.nn.gelu(lgt_c) * (hf_c + hb_c).astype(dt)
        yd_c = _retention_out(o_c, rgt_c)
        ctx = ctx + cg1 * _merge_branches((ya_c, yb_c, yc_c, yd_c), gl_c, p['w_branch'], p['w_out'])
        hc = _modulate(_rmsnorm(ctx, p['norm_ffn']), csh2, csc2).reshape(B_ * Lc, D)
        y = _moe_ffn(jnp.concatenate([h, hc], axis=0), *p['moe'])
        ctx = ctx + cg2 * y[B_ * L:].reshape(B_, Lc, D)
    else:
        y = _moe_ffn(h, *p['moe'])
    x = x + g2 * y[:B_ * L].reshape(B_, L, D)
    return x, ctx


def kernel(x, c, ctx, c_ctx, ada_w, ada_b, norm_mix, norm_ffn, w_in, hy_conv_w, hy_conv_b,
           hy_filt_w1, hy_filt_b1, hy_filt_w2, hy_filt_b2, hy_filt_w3, hy_filt_b3, hy_filt_freq,
           hy_filt_out, hy_bias, swa_sink, lru_conv_w, lru_conv_b, lru_wa, lru_ba, lru_wx, lru_bx,
           lru_lambda, ret_log_decay, w_branch, w_out, router_w, router_bias, exp_w_gate, exp_w_up,
           exp_w_down, sh_w_gate, sh_w_up, sh_w_down, final_norm):
    s_c = jax.nn.silu(c)
    s_cc = jax.nn.silu(c_ctx)
    for l in range(DEPTH):
        p = {
            'norm_mix': norm_mix[l], 'norm_ffn': norm_ffn[l], 'w_in': w_in[l],
            'hy_conv_w': hy_conv_w[l], 'hy_conv_b': hy_conv_b[l],
            'hy_filt': (hy_filt_w1[l], hy_filt_b1[l], hy_filt_w2[l], hy_filt_b2[l], hy_filt_w3[l],
                        hy_filt_b3[l], hy_filt_freq[l], hy_filt_out[l]),
            'hy_bias': hy_bias[l], 'swa_sink': swa_sink[l],
            'lru_conv_w': lru_conv_w[l], 'lru_conv_b': lru_conv_b[l],
            'lru_f': (lru_wa[l][0], lru_ba[l][0], lru_wx[l][0], lru_bx[l][0], lru_lambda[l][0]),
            'lru_b': (lru_wa[l][1], lru_ba[l][1], lru_wx[l][1], lru_bx[l][1], lru_lambda[l][1]),
            'ret_log_decay': ret_log_decay[l], 'w_branch': w_branch[l], 'w_out': w_out[l],
            'moe': (router_w[l], router_bias[l], exp_w_gate[l], exp_w_up[l], exp_w_down[l],
                    sh_w_gate[l], sh_w_up[l], sh_w_down[l]),
        }
        mod_x = jnp.split((s_c @ ada_w[l] + ada_b[l])[:, None, :], 6, axis=-1)
        mod_c = jnp.split(s_cc @ ada_w[l] + ada_b[l], 6, axis=-1)
        x, ctx = _hybrid_layer(x, ctx, mod_x, mod_c, p, l < DEPTH - 1)
    return _rmsnorm(x, final_norm)
```
